```python
import jax, jax.numpy as jnp
from jax import lax
import numpy as np

D_MODEL = 1024
BATCH = 8
SEQ = 2048
DEPTH = 1
DEC_BATCH = 128
DEC_SEQ = 8
PAST_LEN = 16384
PAGE_SIZE = 128

D_MIX = D_MODEL
D_A = D_MIX // 2
D_B = D_MIX - D_A
A_HEADS = 4
A_HEAD_DIM = D_A // A_HEADS
A_CONV = 4
A_CHUNK = 128
B_CONV = 31
N_MEM = 256
X_HEADS = 4
X_HEAD_DIM = D_MODEL // X_HEADS
D_FF = 2816
F_CONV = 3
EPS = 1e-6
IN_COLS = 2 * D_A + 2 * A_HEADS + 2 * D_B

kernel_name = "hymba_mlstm_conformer_convffn_step"


def rmsnorm(x, g):
    xf = x.astype(jnp.float32)
    y = xf * lax.rsqrt(jnp.mean(xf * xf, axis=-1, keepdims=True) + EPS)
    return (y * g.astype(jnp.float32)).astype(x.dtype)


def layernorm(x, g, b):
    xf = x.astype(jnp.float32)
    mu = jnp.mean(xf, axis=-1, keepdims=True)
    xc = xf - mu
    y = xc * lax.rsqrt(jnp.mean(xc * xc, axis=-1, keepdims=True) + EPS)
    return (y * g.astype(jnp.float32) + b.astype(jnp.float32)).astype(x.dtype)


def headnorm(h, g):
    return h * lax.rsqrt(jnp.mean(h * h, axis=-1, keepdims=True) + EPS) * g.astype(jnp.float32)


def causal_dwconv(x_ext, w, b):
    c = x_ext.shape[-1]
    y = lax.conv_general_dilated(x_ext, w[:, None, :].astype(x_ext.dtype), window_strides=(1,), padding='VALID',
                                 dimension_numbers=('NWC', 'WIO', 'NWC'), feature_group_count=c)
    return y + b.astype(x_ext.dtype)


def _mlstm_chunk(carry, inp):
    C0, n0, m0 = carry
    q, k, v, logi, logf = inp
    c = q.shape[2]
    F = jnp.cumsum(logf, axis=-1)
    a = logi - F
    m = F + jnp.maximum(m0[..., None], lax.cummax(a, axis=2))
    causal = jnp.tril(jnp.ones((c, c), dtype=bool))
    logD = F[..., :, None] + a[..., None, :] - m[..., :, None]
    Dm = jnp.exp(jnp.where(causal, logD, -jnp.inf))
    inter = jnp.exp(F + m0[..., None] - m)
    s = jnp.einsum('bhtd,bhsd->bhts', q, k) * Dm
    num = jnp.einsum('bhts,bhsv->bhtv', s, v) + inter[..., None] * jnp.einsum('bhvk,bhtk->bhtv', C0, q)
    den = jnp.sum(s, axis=-1) + inter * jnp.einsum('bhk,bhtk->bht', n0, q)
    h = num / jnp.maximum(jnp.abs(den), jnp.exp(-m))[..., None]
    mL = m[..., -1]
    wdec = jnp.exp(F[..., -1:] + a - mL[..., None])
    cdec = jnp.exp(F[..., -1] + m0 - mL)
    C1 = cdec[..., None, None] * C0 + jnp.einsum('bhs,bhsv,bhsk->bhvk', wdec, v, k)
    n1 = cdec[..., None] * n0 + jnp.einsum('bhs,bhsk->bhk', wdec, k)
    return (C1, n1, mL), h


def mlstm_chunkwise(q, k, v, logi, logf, C0, n0, m0):
    B, H, L, D = q.shape
    c = A_CHUNK if L % A_CHUNK == 0 else L
    N = L // c

    def split(t):
        return jnp.moveaxis(t.reshape((B, H, N, c) + t.shape[3:]), 2, 0)

    (C1, n1, m1), h = lax.scan(_mlstm_chunk, (C0, n0, m0),
                               (split(q), split(k), split(v), split(logi), split(logf)))
    h = jnp.moveaxis(h, 0, 2).reshape(B, H, L, D)
    return h, C1, n1, m1


def hybrid_layer(x, mem_k, mem_v, a_buf, b_buf, f_buf, C0, n0, m0, w):
    B, L, _ = x.shape
    f32 = jnp.float32
    h = rmsnorm(x, w['norm_mix'])
    z = h @ w['w_in']
    o1, o2 = D_A, 2 * D_A
    o3, o4 = o2 + A_HEADS, o2 + 2 * A_HEADS
    o5 = o4 + D_B
    a_x, a_o, a_i, a_f = z[..., :o1], z[..., o1:o2], z[..., o2:o3], z[..., o3:o4]
    b_u, b_g = z[..., o4:o5], z[..., o5:]
    a_ext = jnp.concatenate([a_buf.astype(a_x.dtype), a_x], axis=1)
    a_c = jax.nn.silu(causal_dwconv(a_ext, w['a_conv_w'], w['a_conv_b']))
    ac_h = a_c.reshape(B, L, A_HEADS, A_HEAD_DIM).astype(f32)
    ax_h = a_x.reshape(B, L, A_HEADS, A_HEAD_DIM).astype(f32)
    q = jnp.einsum('blhd,hde->bhle', ac_h, w['a_wq'].astype(f32))
    k = jnp.einsum('blhd,hde->bhle', ac_h, w['a_wk'].astype(f32)) * (A_HEAD_DIM ** -0.5)
    v = jnp.einsum('blhd,hde->bhle', ax_h, w['a_wv'].astype(f32))
    logi = jnp.transpose((a_i + w['a_bi']).astype(f32), (0, 2, 1))
    logf = jnp.transpose(jax.nn.log_sigmoid((a_f + w['a_bf']).astype(f32)), (0, 2, 1))
    ha, C1, n1, m1 = mlstm_chunkwise(q, k, v, logi, logf,
                                     C0.astype(f32), n0.astype(f32), m0.astype(f32))
    ha = headnorm(jnp.transpose(ha, (0, 2, 1, 3)), w['a_hnorm']).reshape(B, L, D_A)
    ha = jax.nn.sigmoid(a_o) * ha.astype(x.dtype)
    u = b_u * jax.nn.sigmoid(b_g)
    b_ext = jnp.concatenate([b_buf.astype(u.dtype), u], axis=1)
    cb = jax.nn.silu(layernorm(causal_dwconv(b_ext, w['b_conv_w'], w['b_conv_b']), w['b_ln_g'], w['b_ln_b']))
    x = x + jnp.concatenate([ha, cb], axis=-1) @ w['w_out']
    hc = rmsnorm(x, w['norm_x'])
    qx = (hc @ w['x_wq']).reshape(B, L, X_HEADS, X_HEAD_DIM)
    sc = jnp.einsum('blhd,bmhd->bhlm', qx, mem_k.astype(qx.dtype)).astype(f32) * (X_HEAD_DIM ** -0.5)
    pr = jax.nn.softmax(sc, axis=-1).astype(x.dtype)
    ox = jnp.einsum('bhlm,bmhd->blhd', pr, mem_v.astype(x.dtype)).reshape(B, L, D_MODEL)
    x = x + ox @ w['x_wo']
    hf = rmsnorm(x, w['norm_ffn'])
    uf = hf @ w['f_wup']
    fa, fg = uf[..., :D_FF], uf[..., D_FF:]
    f_ext = jnp.concatenate([f_buf.astype(fa.dtype), fa], axis=1)
    fa_c = causal_dwconv(f_ext, w['f_conv_w'], w['f_conv_b'])
    x = x + (jax.nn.gelu(fa_c) * fg) @ w['f_wdown']
    new = (C1, n1, m1, a_ext[:, -(A_CONV - 1):], b_ext[:, -(B_CONV - 1):], f_ext[:, -(F_CONV - 1):])
    return x, new


def setup_inputs(seed: int = 0) -> dict:
    key = jax.random.key(seed)
    ks = iter(jax.random.split(key, 48))

    def nrm(shape, scale=1.0):
        return jax.random.normal(next(ks), shape, jnp.float32) * scale

    def gain(shape):
        return 1.0 + nrm(shape, 0.01)

    L = DEPTH
    return {
        "x_prompt": nrm((BATCH, SEQ, D_MODEL)),
        "x_sample": nrm((DEC_BATCH, DEC_SEQ, D_MODEL)),
        "mem_prompt": nrm((BATCH, N_MEM, D_MODEL)),
        "state_mlstm_C": nrm((L, DEC_BATCH, A_HEADS, A_HEAD_DIM, A_HEAD_DIM), 0.1),
        "state_mlstm_n": nrm((L, DEC_BATCH, A_HEADS, A_HEAD_DIM), 0.5),
        "state_mlstm_m": jax.random.uniform(next(ks), (L, DEC_BATCH, A_HEADS), jnp.float32, 0.0, 2.0),
        "state_mlstm_conv": nrm((L, DEC_BATCH, A_CONV - 1, D_A)),
        "state_conv": nrm((L, DEC_BATCH, B_CONV - 1, D_B), 0.5),
        "state_ffn_conv": nrm((L, DEC_BATCH, F_CONV - 1, D_FF)),
        "cache_mem_k": nrm((L, DEC_BATCH, N_MEM, X_HEADS, X_HEAD_DIM)),
        "cache_mem_v": nrm((L, DEC_BATCH, N_MEM, X_HEADS, X_HEAD_DIM)),
        "norm_mix": gain((L, D_MODEL)),
        "w_in": nrm((L, D_MODEL, IN_COLS), D_MODEL ** -0.5),
        "a_conv_w": nrm((L, A_CONV, D_A), A_CONV ** -0.5),
        "a_conv_b": nrm((L, D_A), 0.01),
        "a_wq": nrm((L, A_HEADS, A_HEAD_DIM, A_HEAD_DIM), A_HEAD_DIM ** -0.5),
        "a_wk": nrm((L, A_HEADS, A_HEAD_DIM, A_HEAD_DIM), A_HEAD_DIM ** -0.5),
        "a_wv": nrm((L, A_HEADS, A_HEAD_DIM, A_HEAD_DIM), A_HEAD_DIM ** -0.5),
        "a_bi": nrm((L, A_HEADS), 0.1),
        "a_bf": jnp.broadcast_to(jnp.linspace(3.0, 6.0, A_HEADS, dtype=jnp.float32), (L, A_HEADS)) + nrm((L, A_HEADS), 0.01),
        "a_hnorm": gain((L, A_HEADS, A_HEAD_DIM)),
        "b_conv_w": nrm((L, B_CONV, D_B), B_CONV ** -0.5),
        "b_conv_b": nrm((L, D_B), 0.01),
        "b_ln_g": gain((L, D_B)),
        "b_ln_b": nrm((L, D_B), 0.01),
        "w_out": nrm((L, D_MIX, D_MODEL), D_MIX ** -0.5),
        "norm_x": gain((L, D_MODEL)),
        "norm_mem": gain((L, D_MODEL)),
        "x_wq": nrm((L, D_MODEL, D_MODEL), D_MODEL ** -0.5),
        "x_wk": nrm((L, D_MODEL, D_MODEL), D_MODEL ** -0.5),
        "x_wv": nrm((L, D_MODEL, D_MODEL), D_MODEL ** -0.5),
        "x_wo": nrm((L, D_MODEL, D_MODEL), D_MODEL ** -0.5),
        "norm_ffn": gain((L, D_MODEL)),
        "f_wup": nrm((L, D_MODEL, 2 * D_FF), D_MODEL ** -0.5),
        "f_conv_w": nrm((L, F_CONV, D_FF), F_CONV ** -0.5),
        "f_conv_b": nrm((L, D_FF), 0.01),
        "f_wdown": nrm((L, D_FF, D_MODEL), D_FF ** -0.5),
        "norm_final": gain((D_MODEL,)),
    }


def reference(x_prompt, x_sample, mem_prompt, state_mlstm_C, state_mlstm_n, state_mlstm_m,
              state_mlstm_conv, state_conv, state_ffn_conv, cache_mem_k, cache_mem_v,
              norm_mix, w_in, a_conv_w, a_conv_b, a_wq, a_wk, a_wv, a_bi, a_bf, a_hnorm,
              b_conv_w, b_conv_b, b_ln_g, b_ln_b, w_out, norm_x, norm_mem, x_wq, x_wk, x_wv, x_wo,
              norm_ffn, f_wup, f_conv_w, f_conv_b, f_wdown, norm_final):
    xp, xs = x_prompt, x_sample
    dt = x_prompt.dtype
    Bp = x_prompt.shape[0]
    Ps = [[] for _ in range(8)]
    Ss = [[] for _ in range(6)]
    for l in range(DEPTH):
        w = dict(norm_mix=norm_mix[l], w_in=w_in[l], a_conv_w=a_conv_w[l], a_conv_b=a_conv_b[l],
                 a_wq=a_wq[l], a_wk=a_wk[l], a_wv=a_wv[l], a_bi=a_bi[l], a_bf=a_bf[l], a_hnorm=a_hnorm[l],
                 b_conv_w=b_conv_w[l], b_conv_b=b_conv_b[l], b_ln_g=b_ln_g[l], b_ln_b=b_ln_b[l],
                 w_out=w_out[l], norm_x=norm_x[l], x_wq=x_wq[l], x_wo=x_wo[l],
                 norm_ffn=norm_ffn[l], f_wup=f_wup[l], f_conv_w=f_conv_w[l], f_conv_b=f_conv_b[l],
                 f_wdown=f_wdown[l])
        mn = rmsnorm(mem_prompt, norm_mem[l])
        mk = (mn @ x_wk[l]).reshape(Bp, N_MEM, X_HEADS, X_HEAD_DIM)
        mv = (mn @ x_wv[l]).reshape(Bp, N_MEM, X_HEADS, X_HEAD_DIM)
        xp, newp = hybrid_layer(
            xp, mk, mv,
            jnp.zeros((Bp, A_CONV - 1, D_A), dt), jnp.zeros((Bp, B_CONV - 1, D_B), dt),
            jnp.zeros((Bp, F_CONV - 1, D_FF), dt),
            jnp.zeros((Bp, A_HEADS, A_HEAD_DIM, A_HEAD_DIM), jnp.float32),
            jnp.zeros((Bp, A_HEADS, A_HEAD_DIM), jnp.float32),
            jnp.zeros((Bp, A_HEADS), jnp.float32), w)
        for i in range(6):
            Ps[i].append(newp[i])
        Ps[6].append(mk)
        Ps[7].append(mv)
        xs, news = hybrid_layer(
            xs, cache_mem_k[l], cache_mem_v[l],
            state_mlstm_conv[l], state_conv[l], state_ffn_conv[l],
            state_mlstm_C[l], state_mlstm_n[l], state_mlstm_m[l], w)
        for i in range(6):
            Ss[i].append(news[i])
    y_prompt = rmsnorm(xp, norm_final)
    y_sample = rmsnorm(xs, norm_final)
    C_p, n_p, m_p, aconv_p, conv_p, fconv_p, memk_p, memv_p = [jnp.stack(t, axis=0) for t in Ps]
    C_s, n_s, m_s, aconv_s, conv_s, fconv_s = [jnp.stack(t, axis=0) for t in Ss]
    return (y_prompt, y_sample, C_p, n_p, m_p, aconv_p, conv_p, fconv_p, memk_p, memv_p,
            C_s, n_s, m_s, aconv_s, conv_s, fconv_s)
```

```python
import functools

import jax
import jax.numpy as jnp
from jax import lax
from jax.experimental import pallas as pl
from jax.experimental.pallas import tpu as pltpu

F32 = jnp.float32
BF16 = jnp.bfloat16
EPS = 1e-6

SUBLANES = 8
LANES = 128
VMEM_LIMIT_BYTES = 60000 * 1024

A_HEADS = 4
A_CHUNK = 128
X_HEADS = 4
A_CONV = 4
B_CONV = 31
F_CONV = 3
A_HP = 8
B_HP = 32
F_HP = 8
PROMPT_TILE = 256
CONV_ROWS = 32
FFN_COLS = 256


def _rms(x, g):
    return x * lax.rsqrt(jnp.mean(x * x, axis=-1, keepdims=True) + EPS) * g


def _dot(a, b):
    return jnp.dot(a.astype(BF16), b.astype(BF16), preferred_element_type=F32)


def _dot_nt(a, b):
    return lax.dot_general(a.astype(BF16), b.astype(BF16), (((1,), (1,)), ((), ())), preferred_element_type=F32)


def _sigmoid(x):
    return jax.nn.sigmoid(x)


def _scan(x, axis, op, fill):
    n = x.shape[axis]
    idx = lax.broadcasted_iota(jnp.int32, x.shape, axis)
    sh = 1
    while sh < n:
        x = op(x, jnp.where(idx >= sh, pltpu.roll(x, sh, axis=axis), fill))
        sh *= 2
    return x


def _conv_taps(ext_ref, out_ref, w_ref, b_ref, width, hp, lt, bb, rb):
    bt = ext_ref.shape[0]
    base = hp - (width - 1)
    w = w_ref[...]
    b = b_ref[...]
    for b0 in range(0, bt, bb):
        for r0 in range(0, lt, rb):
            acc = None
            for k in range(width):
                term = w[k:k + 1][None] * ext_ref[b0:b0 + bb, base + k + r0:base + k + r0 + rb, :]
                acc = term if acc is None else acc + term
            out_ref[b0:b0 + bb, r0:r0 + rb, :] = acc + b[None]


def _in_proj(x, nmix_ref, wmain_ref, wg_ref, gbias_ref):
    hb = _rms(x, nmix_ref[...]).astype(BF16)
    c = wmain_ref.shape[1] // 4
    parts = [jnp.dot(hb, wmain_ref[:, i * c:(i + 1) * c], preferred_element_type=F32) for i in range(4)]
    g = jnp.dot(hb, wg_ref[...], preferred_element_type=F32) + gbias_ref[...]
    lane = lax.broadcasted_iota(jnp.int32, g.shape, 1)
    g2 = jnp.where(lane < A_HEADS, g, jax.nn.log_sigmoid(g))
    return parts[0], parts[1], parts[2], parts[3], g2


def _layernorm_silu(y, g, b):
    mu = jnp.mean(y, axis=-1, keepdims=True)
    yc = y - mu
    z = yc * lax.rsqrt(jnp.mean(yc * yc, axis=-1, keepdims=True) + EPS) * g + b
    return z * _sigmoid(z)


def _head_out(hh, gain, a_o):
    hn = hh * lax.rsqrt(jnp.mean(hh * hh, axis=-1, keepdims=True) + EPS) * gain
    return _sigmoid(a_o) * hn


def _attention(qx, k_of_head, v_of_head):
    dh = qx.shape[1] // X_HEADS
    outs = []
    for h in range(X_HEADS):
        sc = _dot_nt(qx[:, h * dh:(h + 1) * dh], k_of_head(h)) * (dh ** -0.5)
        e = jnp.exp(sc - jnp.max(sc, axis=-1, keepdims=True))
        p = e / jnp.sum(e, axis=-1, keepdims=True)
        outs.append(_dot(p, v_of_head(h)))
    return jnp.concatenate(outs, axis=-1)


def _ffn(x2, fext_ref, fa_ref, act_ref, nffn_ref, fwup_ref, fcw_ref, fcb_ref, fwdown_ref, lt):
    bt = fext_ref.shape[0]
    dff = fext_ref.shape[2]
    hb = _rms(x2, nffn_ref[...]).astype(BF16)
    fcw = fcw_ref[...]
    fcb = fcb_ref[...]
    for c0 in range(0, dff, FFN_COLS):
        cs = slice(c0, c0 + FFN_COLS)
        fa = jnp.dot(hb, fwup_ref[:, cs], preferred_element_type=F32)
        fg = jnp.dot(hb, fwup_ref[:, dff + c0:dff + c0 + FFN_COLS], preferred_element_type=F32)
        fext_ref[:, F_HP:F_HP + lt, cs] = fa.reshape(bt, lt, FFN_COLS)
        conv = fcb[:, cs][None]
        for k in range(F_CONV):
            off = F_HP - (F_CONV - 1) + k
            conv = conv + fcw[k:k + 1, cs][None] * fext_ref[:, off:off + lt, cs]
        act = jax.nn.gelu(conv.reshape(bt * lt, FFN_COLS)) * fg
        act_ref[:, cs] = act.astype(BF16)
    del fa_ref
    return x2 + jnp.dot(act_ref[...], fwdown_ref[...], preferred_element_type=F32)


def _memkv_kernel(mem_ref, nmem_ref, wk_ref, wv_ref, mk_ref, mv_ref, kb_ref, vb_ref):
    mn = _rms(mem_ref[...], nmem_ref[...]).astype(BF16)
    mk = jnp.dot(mn, wk_ref[...], preferred_element_type=F32)
    mv = jnp.dot(mn, wv_ref[...], preferred_element_type=F32)
    mk_ref[...] = mk
    mv_ref[...] = mv
    kb_ref[...] = mk.astype(BF16)
    vb_ref[...] = mv.astype(BF16)


def _const_spec(shape):
    nd = len(shape)
    return pl.BlockSpec(shape, lambda *_: (0,) * nd, pipeline_mode=pl.Buffered(1))


def _memkv(mem2d, nmem, wk, wv):
    m, d = mem2d.shape
    tm = 512
    row = pl.BlockSpec((tm, d), lambda i: (i, 0))
    return pl.pallas_call(
        _memkv_kernel,
        grid=(m // tm,),
        in_specs=[row, _const_spec(nmem.shape), _const_spec(wk.shape), _const_spec(wv.shape)],
        out_specs=[row, row, row, row],
        out_shape=[jax.ShapeDtypeStruct((m, d), F32), jax.ShapeDtypeStruct((m, d), F32),
                   jax.ShapeDtypeStruct((m, d), BF16), jax.ShapeDtypeStruct((m, d), BF16)],
        compiler_params=pltpu.CompilerParams(dimension_semantics=("arbitrary",), vmem_limit_bytes=VMEM_LIMIT_BYTES),
        name="memkv",
    )(mem2d, nmem, wk, wv)


def _mlstm_chunk_prompt(rows, q_ref, k_ref, v_ref, g_ref, ao_ref, ha_ref, ahn_ref, c_s, n_s, m_s):
    c = A_CHUNK
    dh = LANES
    g8 = g_ref[rows, :].T[0:SUBLANES, :]
    cs = _scan(g8, 1, jnp.add, 0.0)
    f = pltpu.roll(cs, A_HEADS, axis=0)
    a = g8 - f
    cm = _scan(a, 1, jnp.maximum, -jnp.inf)
    m0 = m_s[...]
    m = f + jnp.maximum(m0, cm)
    dlog = f - m
    inter = jnp.exp(f + m0 - m)
    em = jnp.exp(-m)
    f_last = f[:, c - 1:c]
    m_last = m[:, c - 1:c]
    wdec = jnp.exp(f_last + a - m_last)
    cdec = jnp.exp(f_last + m0 - m_last)
    row8 = lax.broadcasted_iota(jnp.int32, (SUBLANES, c), 0)
    pack_a = jnp.where(row8 < A_HEADS, dlog, pltpu.roll(inter, A_HEADS, axis=0))
    pack_b = jnp.where(row8 < A_HEADS, em, pltpu.roll(wdec, A_HEADS, axis=0))
    cols = jnp.concatenate([pack_a, pack_b, jnp.zeros((c - 2 * SUBLANES, c), F32)], axis=0).T
    t_idx = lax.broadcasted_iota(jnp.int32, (c, c), 0)
    s_idx = lax.broadcasted_iota(jnp.int32, (c, c), 1)
    causal = t_idx >= s_idx
    for h in range(A_HEADS):
        hs = slice(h * dh, (h + 1) * dh)
        qh = q_ref[rows, hs]
        kh = k_ref[rows, hs]
        vh = v_ref[rows, hs]
        dcol = cols[:, h:h + 1]
        icol = cols[:, A_HEADS + h:A_HEADS + h + 1]
        emcol = cols[:, 2 * A_HEADS + h:2 * A_HEADS + h + 1]
        wcol = cols[:, 3 * A_HEADS + h:3 * A_HEADS + h + 1]
        dm = jnp.exp(jnp.where(causal, dcol + a[h:h + 1, :], -jnp.inf))
        s = _dot_nt(qh, kh) * dm
        c0 = c_s[h]
        n0 = n_s[h:h + 1, :]
        num = _dot(s, vh) + icol * _dot_nt(qh, c0)
        den = jnp.sum(s, axis=-1, keepdims=True) + icol * jnp.sum(qh * n0, axis=-1, keepdims=True)
        hh = num / jnp.maximum(jnp.abs(den), emcol)
        cd = cdec[h:h + 1, :]
        c_s[h] = cd * c0 + _dot((vh * wcol).T, kh)
        n_s[h:h + 1, :] = cd * n0 + jnp.sum(kh * wcol, axis=0, keepdims=True)
        ha_ref[rows, hs] = _head_out(hh, ahn_ref[:, hs], ao_ref[rows, hs])
    m_s[...] = jnp.broadcast_to(m_last, (SUBLANES, c))


def _prompt_kernel(
        x_ref, kb_ref, vb_ref,
        nmix_ref, wmain_ref, wg_ref, gbias_ref, acw_ref, acb_ref, wq_ref, wk_ref, wv_ref, ahn_ref,
        bcw_ref, bcb_ref, blg_ref, blb_ref, wout_ref, nx_ref, xwq_ref, xwo_ref,
        nffn_ref, fwup_ref, fcw_ref, fcb_ref, fwdown_ref, nfin_ref,
        y_ref, c_out, n_out, m_out, ac_out, bc_out, fc_out,
        c_s, n_s, m_s, aext, bext, fext, q_s, k_s, v_s, g_s, ao_s, ha_s, cv_s, act_s):
    tl = x_ref.shape[1]
    l = pl.program_id(1)
    last = l == pl.num_programs(1) - 1

    @pl.when(l == 0)
    def _():
        c_s[...] = jnp.zeros_like(c_s)
        n_s[...] = jnp.zeros_like(n_s)
        m_s[...] = jnp.zeros_like(m_s)
        aext[:, 0:A_HP, :] = jnp.zeros((1, A_HP, aext.shape[2]), F32)
        bext[:, 0:B_HP, :] = jnp.zeros((1, B_HP, bext.shape[2]), F32)
        fext[:, 0:F_HP, :] = jnp.zeros((1, F_HP, fext.shape[2]), F32)

    x = x_ref[0]
    a_x, a_o, b_u, b_g, g2 = _in_proj(x, nmix_ref, wmain_ref, wg_ref, gbias_ref)
    g_s[...] = g2
    ao_s[...] = a_o

    aext[:, A_HP:A_HP + tl, :] = a_x[None]
    _conv_taps(aext, cv_s, acw_ref, acb_ref, A_CONV, A_HP, tl, 1, CONV_ROWS)
    pre = cv_s[0]
    a_c = (pre * _sigmoid(pre)).astype(BF16)
    q_s[...] = jnp.dot(a_c, wq_ref[...], preferred_element_type=F32)
    k_s[...] = jnp.dot(a_c, wk_ref[...], preferred_element_type=F32) * (LANES ** -0.5)
    v_s[...] = jnp.dot(a_x.astype(BF16), wv_ref[...], preferred_element_type=F32)
    a_hist = aext[:, A_HP + tl - (A_CONV - 1):A_HP + tl, :]
    aext[:, A_HP - (A_CONV - 1):A_HP, :] = a_hist
    for j in range(tl // A_CHUNK):
        _mlstm_chunk_prompt(slice(j * A_CHUNK, (j + 1) * A_CHUNK), q_s, k_s, v_s, g_s, ao_s, ha_s, ahn_ref,
                            c_s, n_s, m_s)

    bext[:, B_HP:B_HP + tl, :] = (b_u * _sigmoid(b_g))[None]
    _conv_taps(bext, cv_s, bcw_ref, bcb_ref, B_CONV, B_HP, tl, 1, CONV_ROWS)
    cb = _layernorm_silu(cv_s[0], blg_ref[...], blb_ref[...])
    b_hist = bext[:, B_HP + tl - (B_CONV - 1):B_HP + tl, :]
    bext[:, B_HP - (B_CONV - 1):B_HP, :] = b_hist
    d_a = ha_s.shape[1]
    x1 = x + _dot(ha_s[...], wout_ref[0:d_a, :]) + _dot(cb, wout_ref[d_a:, :])

    qx = _dot(_rms(x1, nx_ref[...]), xwq_ref[...])
    dh = qx.shape[1] // X_HEADS
    ox = _attention(qx, lambda h: kb_ref[0, :, h * dh:(h + 1) * dh], lambda h: vb_ref[0, :, h * dh:(h + 1) * dh])
    x2 = x1 + _dot(ox, xwo_ref[...])

    x3 = _ffn(x2, fext, None, act_s, nffn_ref, fwup_ref, fcw_ref, fcb_ref, fwdown_ref, tl)
    f_hist = fext[:, F_HP + tl - (F_CONV - 1):F_HP + tl, :]
    fext[:, F_HP - (F_CONV - 1):F_HP, :] = f_hist
    y_ref[0] = _rms(x3, nfin_ref[...])

    @pl.when(last)
    def _():
        c_out[0] = c_s[...]
        n_out[0] = n_s[0:A_HEADS, :]
        m_out[0] = m_s[...]
        ac_out[...] = a_hist
        bc_out[...] = b_hist
        fc_out[...] = f_hist


def _prompt_layer(x, kb, vb, w):
    b, l, d = x.shape
    tl = PROMPT_TILE
    n_mem = kb.shape[1]
    d_a = w["wq"].shape[0]
    d_ff = w["fwdown"].shape[0]
    names = ["nmix", "wmain", "wg", "gbias", "acw", "acb", "wq", "wk", "wv", "ahn", "bcw", "bcb", "blg", "blb",
             "wout", "nx", "xwq", "xwo", "nffn", "fwup", "fcw", "fcb", "fwdown", "nfin"]
    weights = [w[n] for n in names]
    per_b3 = lambda shape: pl.BlockSpec((1,) + shape, lambda i, j: (i, 0, 0))
    in_specs = [pl.BlockSpec((1, tl, d), lambda i, j: (i, j, 0)), per_b3((n_mem, d)), per_b3((n_mem, d))]
    in_specs += [_const_spec(a.shape) for a in weights]
    out_specs = [
        pl.BlockSpec((1, tl, d), lambda i, j: (i, j, 0)),
        pl.BlockSpec((1, A_HEADS, LANES, LANES), lambda i, j: (i, 0, 0, 0)),
        per_b3((A_HEADS, LANES)),
        per_b3((SUBLANES, LANES)),
        per_b3((A_CONV - 1, d_a)),
        per_b3((B_CONV - 1, d_a)),
        per_b3((F_CONV - 1, d_ff)),
    ]
    out_shape = [
        jax.ShapeDtypeStruct((b, l, d), F32),
        jax.ShapeDtypeStruct((b, A_HEADS, LANES, LANES), F32),
        jax.ShapeDtypeStruct((b, A_HEADS, LANES), F32),
        jax.ShapeDtypeStruct((b, SUBLANES, LANES), F32),
        jax.ShapeDtypeStruct((b, A_CONV - 1, d_a), F32),
        jax.ShapeDtypeStruct((b, B_CONV - 1, d_a), F32),
        jax.ShapeDtypeStruct((b, F_CONV - 1, d_ff), F32),
    ]
    scratch = [
        pltpu.VMEM((A_HEADS, LANES, LANES), F32),
        pltpu.VMEM((SUBLANES, LANES), F32),
        pltpu.VMEM((SUBLANES, LANES), F32),
        pltpu.VMEM((1, A_HP + tl, d_a), F32),
        pltpu.VMEM((1, B_HP + tl, d_a), F32),
        pltpu.VMEM((1, F_HP + tl, d_ff), F32),
        pltpu.VMEM((tl, d_a), F32),
        pltpu.VMEM((tl, d_a), F32),
        pltpu.VMEM((tl, d_a), F32),
        pltpu.VMEM((tl, LANES), F32),
        pltpu.VMEM((tl, d_a), F32),
        pltpu.VMEM((tl, d_a), F32),
        pltpu.VMEM((1, tl, d_a), F32),
        pltpu.VMEM((tl, d_ff), BF16),
    ]
    return pl.pallas_call(
        _prompt_kernel,
        grid=(b, l // tl),
        in_specs=in_specs,
        out_specs=out_specs,
        out_shape=out_shape,
        scratch_shapes=scratch,
        compiler_params=pltpu.CompilerParams(dimension_semantics=("arbitrary", "arbitrary"),
                                             vmem_limit_bytes=VMEM_LIMIT_BYTES),
        name="prompt_layer",
    )(x, kb, vb, *weights)


def _mlstm_seq_sample(b, lt, q_ref, k_ref, v_ref, g_ref, ao_ref, ha_ref, ahn_ref, c0_ref, n0_ref, m0_ref,
                      c1_ref, n1_ref, m1_ref):
    dh = LANES
    rows = pl.ds(pl.multiple_of(b * lt, lt), lt)
    g8 = g_ref[rows, :]
    cs = _scan(g8, 0, jnp.add, 0.0)
    f = pltpu.roll(cs, LANES - A_HEADS, axis=1)
    a = g8 - f
    cm = _scan(a, 0, jnp.maximum, -jnp.inf)
    m0 = m0_ref[b]
    m = f + jnp.maximum(m0, cm)
    dlog = f - m
    inter = jnp.exp(f + m0 - m)
    em = jnp.exp(-m)
    f_last = f[lt - 1:lt, :]
    m_last = m[lt - 1:lt, :]
    wdec = jnp.exp(f_last + a - m_last)
    cdec = jnp.exp(f_last + m0 - m_last)
    m1_ref[b] = m_last
    pad = jnp.zeros((LANES - lt, LANES), F32)
    a_t = jnp.concatenate([a, pad], axis=0).T
    t_idx = lax.broadcasted_iota(jnp.int32, (lt, LANES), 0)
    s_idx = lax.broadcasted_iota(jnp.int32, (lt, LANES), 1)
    causal = t_idx >= s_idx
    for h in range(A_HEADS):
        hs = slice(h * dh, (h + 1) * dh)
        qh = q_ref[rows, hs]
        kh = k_ref[rows, hs]
        vh = v_ref[rows, hs]
        kpad = jnp.concatenate([kh, pad], axis=0)
        vpad = jnp.concatenate([vh, pad], axis=0)
        dm = jnp.exp(jnp.where(causal, dlog[:, h:h + 1] + a_t[h:h + 1, :], -jnp.inf))
        s = _dot_nt(qh, kpad) * dm
        c0 = c0_ref[b, h]
        n0 = n0_ref[b, h:h + 1, :]
        icol = inter[:, h:h + 1]
        num = _dot(s, vpad) + icol * _dot_nt(qh, c0)
        den = jnp.sum(s, axis=-1, keepdims=True) + icol * jnp.sum(qh * n0, axis=-1, keepdims=True)
        hh = num / jnp.maximum(jnp.abs(den), em[:, h:h + 1])
        cd = cdec[:, h:h + 1]
        wcol = wdec[:, h:h + 1]
        vw_t = jnp.concatenate([vh * wcol, pad], axis=0).T
        c1_ref[b, h] = cd * c0 + _dot(vw_t, kpad)
        n1_ref[b, h:h + 1, :] = cd * n0 + jnp.sum(kh * wcol, axis=0, keepdims=True)
        ha_ref[rows, hs] = _head_out(hh, ahn_ref[:, hs], ao_ref[rows, hs])


def _sample_mix_kernel(
        x_ref, c0_ref, n0_ref, m0_ref, ac_in, bc_in,
        nmix_ref, wmain_ref, wg_ref, gbias_ref, acw_ref, acb_ref, wq_ref, wk_ref, wv_ref, ahn_ref,
        bcw_ref, bcb_ref, blg_ref, blb_ref, wout_ref, nx_ref, xwq_ref,
        x1_ref, qx_ref, c1_ref, n1_ref, m1_ref, ac_out, bc_out,
        aext, bext, q_s, k_s, v_s, g_s, ao_s, ha_s, cva_s, cvb_s):
    bt = c0_ref.shape[0]
    rows = x_ref.shape[0]
    lt = rows // bt
    d_a = ha_s.shape[1]
    x = x_ref[...]
    a_x, a_o, b_u, b_g, g2 = _in_proj(x, nmix_ref, wmain_ref, wg_ref, gbias_ref)
    g_s[...] = g2
    ao_s[...] = a_o

    aext[:, A_HP - (A_CONV - 1):A_HP, :] = ac_in[...]
    aext[:, A_HP:A_HP + lt, :] = a_x.reshape(bt, lt, d_a)
    _conv_taps(aext, cva_s, acw_ref, acb_ref, A_CONV, A_HP, lt, CONV_ROWS // lt, lt)
    ac_out[...] = aext[:, A_HP + lt - (A_CONV - 1):A_HP + lt, :]
    pre = cva_s[...].reshape(rows, d_a)
    a_c = (pre * _sigmoid(pre)).astype(BF16)
    q_s[...] = jnp.dot(a_c, wq_ref[...], preferred_element_type=F32)
    k_s[...] = jnp.dot(a_c, wk_ref[...], preferred_element_type=F32) * (LANES ** -0.5)
    v_s[...] = jnp.dot(a_x.astype(BF16), wv_ref[...], preferred_element_type=F32)

    def seq(b, carry):
        _mlstm_seq_sample(b, lt, q_s, k_s, v_s, g_s, ao_s, ha_s, ahn_ref, c0_ref, n0_ref, m0_ref,
                          c1_ref, n1_ref, m1_ref)
        return carry
    lax.fori_loop(0, bt, seq, 0)

    bext[:, B_HP - (B_CONV - 1):B_HP, :] = bc_in[...]
    bext[:, B_HP:B_HP + lt, :] = (b_u * _sigmoid(b_g)).reshape(bt, lt, d_a)
    _conv_taps(bext, cvb_s, bcw_ref, bcb_ref, B_CONV, B_HP, lt, CONV_ROWS // lt, lt)
    bc_out[...] = bext[:, B_HP + lt - (B_CONV - 1):B_HP + lt, :]
    cb = _layernorm_silu(cvb_s[...].reshape(rows, d_a), blg_ref[...], blb_ref[...])
    x1 = x + _dot(ha_s[...], wout_ref[0:d_a, :]) + _dot(cb, wout_ref[d_a:, :])
    x1_ref[...] = x1
    qx_ref[...] = _dot(_rms(x1, nx_ref[...]), xwq_ref[...])


def _sample_mix(x2d, c0, n0, m0p, ac0, bc0, w, bt):
    rows_total, d = x2d.shape
    nb = c0.shape[0]
    lt = rows_total // nb
    rows = bt * lt
    d_a = w["wq"].shape[0]
    names = ["nmix", "wmain", "wg", "gbias", "acw", "acb", "wq", "wk", "wv", "ahn", "bcw", "bcb", "blg", "blb",
             "wout", "nx", "xwq"]
    weights = [w[n] for n in names]
    row_spec = pl.BlockSpec((rows, d), lambda i: (i, 0))
    b3 = lambda shape: pl.BlockSpec((bt,) + shape, lambda i: (i, 0, 0))
    c_spec = pl.BlockSpec((bt, A_HEADS, LANES, LANES), lambda i: (i, 0, 0, 0))
    in_specs = [row_spec, c_spec, b3((A_HEADS, LANES)), b3((1, LANES)), b3((A_CONV - 1, d_a)), b3((B_CONV - 1, d_a))]
    in_specs += [_const_spec(a.shape) for a in weights]
    out_specs = [row_spec, row_spec, c_spec, b3((A_HEADS, LANES)), b3((1, LANES)),
                 b3((A_CONV - 1, d_a)), b3((B_CONV - 1, d_a))]
    out_shape = [
        jax.ShapeDtypeStruct((rows_total, d), F32),
        jax.ShapeDtypeStruct((rows_total, d), F32),
        jax.ShapeDtypeStruct(c0.shape, F32),
        jax.ShapeDtypeStruct(n0.shape, F32),
        jax.ShapeDtypeStruct(m0p.shape, F32),
        jax.ShapeDtypeStruct(ac0.shape, F32),
        jax.ShapeDtypeStruct(bc0.shape, F32),
    ]
    scratch = [
        pltpu.VMEM((bt, A_HP + lt, d_a), F32),
        pltpu.VMEM((bt, B_HP + lt, d_a), F32),
        pltpu.VMEM((rows, d_a), F32),
        pltpu.VMEM((rows, d_a), F32),
        pltpu.VMEM((rows, d_a), F32),
        pltpu.VMEM((rows, LANES), F32),
        pltpu.VMEM((rows, d_a), F32),
        pltpu.VMEM((rows, d_a), F32),
        pltpu.VMEM((bt, lt, d_a), F32),
        pltpu.VMEM((bt, lt, d_a), F32),
    ]
    return pl.pallas_call(
        _sample_mix_kernel,
        grid=(nb // bt,),
        in_specs=in_specs,
        out_specs=out_specs,
        out_shape=out_shape,
        scratch_shapes=scratch,
        compiler_params=pltpu.CompilerParams(dimension_semantics=("arbitrary",), vmem_limit_bytes=VMEM_LIMIT_BYTES),
        name="sample_mix",
    )(x2d, c0, n0, m0p, ac0, bc0, *weights)


def _sample_attn_kernel(qx_ref, k_ref, v_ref, ox_ref):
    bt = k_ref.shape[0]
    lt = qx_ref.shape[0] // bt
    dh = qx_ref.shape[1] // X_HEADS
    for b in range(bt):
        qx = qx_ref[b * lt:(b + 1) * lt, :]
        ox_ref[b * lt:(b + 1) * lt, :] = _attention(
            qx, lambda h: k_ref[b, :, h * dh:(h + 1) * dh], lambda h: v_ref[b, :, h * dh:(h + 1) * dh])


def _sample_attn(qx2d, mem_k, mem_v, bt):
    rows_total, d = qx2d.shape
    nb, n_mem, _ = mem_k.shape
    lt = rows_total // nb
    row_spec = pl.BlockSpec((bt * lt, d), lambda i: (i, 0))
    kv_spec = pl.BlockSpec((bt, n_mem, d), lambda i: (i, 0, 0))
    return pl.pallas_call(
        _sample_attn_kernel,
        grid=(nb // bt,),
        in_specs=[row_spec, kv_spec, kv_spec],
        out_specs=row_spec,
        out_shape=jax.ShapeDtypeStruct((rows_total, d), F32),
        compiler_params=pltpu.CompilerParams(dimension_semantics=("arbitrary",), vmem_limit_bytes=VMEM_LIMIT_BYTES),
        name="sample_attn",
    )(qx2d, mem_k, mem_v)


def _sample_ffn_kernel(x1_ref, ox_ref, fc_in, xwo_ref, nffn_ref, fwup_ref, fcw_ref, fcb_ref, fwdown_ref, nfin_ref,
                       y_ref, fc_out, fext, act_s):
    bt = fc_in.shape[0]
    lt = x1_ref.shape[0] // bt
    x2 = x1_ref[...] + _dot(ox_ref[...], xwo_ref[...])
    fext[:, F_HP - (F_CONV - 1):F_HP, :] = fc_in[...]
    x3 = _ffn(x2, fext, None, act_s, nffn_ref, fwup_ref, fcw_ref, fcb_ref, fwdown_ref, lt)
    fc_out[...] = fext[:, F_HP + lt - (F_CONV - 1):F_HP + lt, :]
    y_ref[...] = _rms(x3, nfin_ref[...])


def _sample_ffn(x1, ox, fc0, w, bt):
    rows_total, d = x1.shape
    nb = fc0.shape[0]
    lt = rows_total // nb
    rows = bt * lt
    d_ff = w["fwdown"].shape[0]
    names = ["xwo", "nffn", "fwup", "fcw", "fcb", "fwdown", "nfin"]
    weights = [w[n] for n in names]
    row_spec = pl.BlockSpec((rows, d), lambda i: (i, 0))
    st_spec = pl.BlockSpec((bt, F_CONV - 1, d_ff), lambda i: (i, 0, 0))
    return pl.pallas_call(
        _sample_ffn_kernel,
        grid=(nb // bt,),
        in_specs=[row_spec, row_spec, st_spec] + [_const_spec(a.shape) for a in weights],
        out_specs=[row_spec, st_spec],
        out_shape=[jax.ShapeDtypeStruct((rows_total, d), F32), jax.ShapeDtypeStruct(fc0.shape, F32)],
        scratch_shapes=[pltpu.VMEM((bt, F_HP + lt, d_ff), F32), pltpu.VMEM((rows, d_ff), BF16)],
        compiler_params=pltpu.CompilerParams(dimension_semantics=("arbitrary",), vmem_limit_bytes=VMEM_LIMIT_BYTES),
        name="sample_ffn",
    )(x1, ox, fc0, *weights)


def _block_diag(wh):
    h, d, _ = wh.shape
    eye = jnp.eye(h, dtype=wh.dtype)
    return (eye[:, None, :, None] * wh[:, :, None, :]).reshape(h * d, h * d)


def _layer_weights(l, norm_mix, w_in, a_conv_w, a_conv_b, a_wq, a_wk, a_wv, a_bi, a_bf, a_hnorm, b_conv_w, b_conv_b,
                   b_ln_g, b_ln_b, w_out, norm_x, x_wq, x_wo, norm_ffn, f_wup, f_conv_w, f_conv_b, f_wdown,
                   norm_final):
    d_a = a_conv_w.shape[2]
    g0 = 2 * d_a
    g1 = g0 + 2 * A_HEADS
    row = lambda v: v.reshape(1, -1).astype(F32)
    wi = w_in[l]
    gbias = jnp.zeros((LANES,), F32).at[0:A_HEADS].set(a_bi[l]).at[A_HEADS:2 * A_HEADS].set(a_bf[l])
    return dict(
        nmix=row(norm_mix[l]),
        wmain=jnp.concatenate([wi[:, :g0], wi[:, g1:]], axis=1).astype(BF16),
        wg=jnp.pad(wi[:, g0:g1], ((0, 0), (0, LANES - 2 * A_HEADS))).astype(BF16),
        gbias=row(gbias),
        acw=a_conv_w[l], acb=row(a_conv_b[l]),
        wq=_block_diag(a_wq[l]).astype(BF16), wk=_block_diag(a_wk[l]).astype(BF16),
        wv=_block_diag(a_wv[l]).astype(BF16),
        ahn=row(a_hnorm[l]),
        bcw=b_conv_w[l], bcb=row(b_conv_b[l]), blg=row(b_ln_g[l]), blb=row(b_ln_b[l]),
        wout=w_out[l].astype(BF16), nx=row(norm_x[l]), xwq=x_wq[l].astype(BF16), xwo=x_wo[l].astype(BF16),
        nffn=row(norm_ffn[l]), fwup=f_wup[l].astype(BF16), fcw=f_conv_w[l], fcb=row(f_conv_b[l]),
        fwdown=f_wdown[l].astype(BF16), nfin=row(norm_final),
    )


def kernel(x_prompt, x_sample, mem_prompt, state_mlstm_C, state_mlstm_n, state_mlstm_m, state_mlstm_conv, state_conv, state_ffn_conv, cache_mem_k, cache_mem_v, norm_mix, w_in, a_conv_w, a_conv_b, a_wq, a_wk, a_wv, a_bi, a_bf, a_hnorm, b_conv_w, b_conv_b, b_ln_g, b_ln_b, w_out, norm_x, norm_mem, x_wq, x_wk, x_wv, x_wo, norm_ffn, f_wup, f_conv_w, f_conv_b, f_wdown, norm_final):
    depth = w_in.shape[0]
    assert depth == 1, "the final norm is fused into the (single) layer"
    bp, lp, d = x_prompt.shape
    bs, ls, _ = x_sample.shape
    n_mem = mem_prompt.shape[1]
    l = 0
    w = _layer_weights(l, norm_mix, w_in, a_conv_w, a_conv_b, a_wq, a_wk, a_wv, a_bi, a_bf, a_hnorm, b_conv_w,
                       b_conv_b, b_ln_g, b_ln_b, w_out, norm_x, x_wq, x_wo, norm_ffn, f_wup, f_conv_w, f_conv_b,
                       f_wdown, norm_final)

    mk, mv, kb, vb = _memkv(mem_prompt.reshape(bp * n_mem, d), norm_mem[l].reshape(1, d),
                            x_wk[l].astype(BF16), x_wv[l].astype(BF16))
    y_p, c_p, n_p, m_p, ac_p, bc_p, fc_p = _prompt_layer(
        x_prompt, kb.reshape(bp, n_mem, d), vb.reshape(bp, n_mem, d), w)
    xh = cache_mem_k.shape[3]
    memk_p = mk.reshape(1, bp, n_mem, xh, d // xh)
    memv_p = mv.reshape(1, bp, n_mem, xh, d // xh)

    m0p = jnp.pad(state_mlstm_m[l], ((0, 0), (0, LANES - A_HEADS)))[:, None, :]
    x1, qx, c_s, n_s, m_s, ac_s, bc_s = _sample_mix(
        x_sample.reshape(bs * ls, d), state_mlstm_C[l], state_mlstm_n[l], m0p, state_mlstm_conv[l], state_conv[l],
        w, bt=16)
    ox = _sample_attn(qx, cache_mem_k[l].reshape(bs, n_mem, d), cache_mem_v[l].reshape(bs, n_mem, d), bt=8)
    y_s, fc_s = _sample_ffn(x1, ox, state_ffn_conv[l], w, bt=32)

    return (y_p, y_s.reshape(bs, ls, d),
            c_p[None], n_p[None], m_p[None, :, 0:A_HEADS, 0], ac_p[None], bc_p[None], fc_p[None], memk_p, memv_p,
            c_s[None], n_s[None], m_s[None, :, 0, 0:A_HEADS], ac_s[None], bc_s[None], fc_s[None])
```

```python
import functools

import jax
import jax.numpy as jnp
from jax import lax
from jax.experimental import pallas as pl
from jax.experimental.pallas import tpu as pltpu

F32 = jnp.float32
BF16 = jnp.bfloat16
EPS = 1e-6

SUBLANES = 8
LANES = 128
VMEM_LIMIT_BYTES = 60000 * 1024

A_HEADS = 4
A_CHUNK = 128
X_HEADS = 4
A_CONV = 4
B_CONV = 31
F_CONV = 3
A_HP = 8
B_HP = 32
F_HP = 8
PROMPT_TILE = 256
CONV_ROWS = 32
FFN_COLS = 256


def _rms(x, g):
    return x * lax.rsqrt(jnp.mean(x * x, axis=-1, keepdims=True) + EPS) * g


def _dot(a, b):
    return jnp.dot(a.astype(BF16), b.astype(BF16), preferred_element_type=F32)


def _dot_nt(a, b):
    return lax.dot_general(a.astype(BF16), b.astype(BF16), (((1,), (1,)), ((), ())), preferred_element_type=F32)


def _sigmoid(x):
    return jax.nn.sigmoid(x)


def _scan(x, axis, op, fill):
    n = x.shape[axis]
    idx = lax.broadcasted_iota(jnp.int32, x.shape, axis)
    sh = 1
    while sh < n:
        x = op(x, jnp.where(idx >= sh, pltpu.roll(x, sh, axis=axis), fill))
        sh *= 2
    return x


def _conv_taps(ext_ref, out_ref, w_ref, b_ref, width, hp, lt, bb, rb):
    bt = ext_ref.shape[0]
    base = hp - (width - 1)
    w = w_ref[...]
    b = b_ref[...]
    for b0 in range(0, bt, bb):
        for r0 in range(0, lt, rb):
            acc = None
            for k in range(width):
                term = w[k:k + 1][None] * ext_ref[b0:b0 + bb, base + k + r0:base + k + r0 + rb, :]
                acc = term if acc is None else acc + term
            out_ref[b0:b0 + bb, r0:r0 + rb, :] = acc + b[None]


def _conv_taps_seq(ext_ref, rot_ref, out_ref, w_ref, b_ref, width, hp, lt, rb):
    n = ext_ref.shape[1]
    base = hp - (width - 1)
    ext = ext_ref[0]
    residues = sorted({(base + k) % SUBLANES for k in range(width)})
    for p in residues:
        rot_ref[p, 0:n, :] = ext if p == 0 else pltpu.roll(ext, n - p, axis=0)
    w = w_ref[...]
    b = b_ref[...]
    for r0 in range(0, lt, rb):
        acc = None
        for k in range(width):
            off = base + k
            start = off - off % SUBLANES + r0
            term = w[k:k + 1] * rot_ref[off % SUBLANES, start:start + rb, :]
            acc = term if acc is None else acc + term
        out_ref[0, r0:r0 + rb, :] = acc + b


def _in_proj(x, nmix_ref, wmain_ref, wg_ref, gbias_ref):
    hb = _rms(x, nmix_ref[...]).astype(BF16)
    c = wmain_ref.shape[1] // 4
    parts = [jnp.dot(hb, wmain_ref[:, i * c:(i + 1) * c], preferred_element_type=F32) for i in range(4)]
    g = jnp.dot(hb, wg_ref[...], preferred_element_type=F32) + gbias_ref[...]
    lane = lax.broadcasted_iota(jnp.int32, g.shape, 1)
    g2 = jnp.where(lane < A_HEADS, g, jax.nn.log_sigmoid(g))
    return parts[0], parts[1], parts[2], parts[3], g2


def _layernorm_silu(y, g, b):
    mu = jnp.mean(y, axis=-1, keepdims=True)
    yc = y - mu
    z = yc * lax.rsqrt(jnp.mean(yc * yc, axis=-1, keepdims=True) + EPS) * g + b
    return z * _sigmoid(z)


def _head_out(hh, gain, a_o):
    hn = hh * lax.rsqrt(jnp.mean(hh * hh, axis=-1, keepdims=True) + EPS) * gain
    return _sigmoid(a_o) * hn


def _attention(qx, k_of_head, v_of_head):
    dh = qx.shape[1] // X_HEADS
    outs = []
    for h in range(X_HEADS):
        sc = _dot_nt(qx[:, h * dh:(h + 1) * dh], k_of_head(h)) * (dh ** -0.5)
        e = jnp.exp(sc - jnp.max(sc, axis=-1, keepdims=True))
        p = e / jnp.sum(e, axis=-1, keepdims=True)
        outs.append(_dot(p, v_of_head(h)))
    return jnp.concatenate(outs, axis=-1)


def _ffn(x2, fext_ref, fa_ref, act_ref, nffn_ref, fwup_ref, fcw_ref, fcb_ref, fwdown_ref, lt):
    bt = fext_ref.shape[0]
    dff = fext_ref.shape[2]
    hb = _rms(x2, nffn_ref[...]).astype(BF16)
    fcw = fcw_ref[...]
    fcb = fcb_ref[...]
    for c0 in range(0, dff, FFN_COLS):
        cs = slice(c0, c0 + FFN_COLS)
        fa = jnp.dot(hb, fwup_ref[:, cs], preferred_element_type=F32)
        fg = jnp.dot(hb, fwup_ref[:, dff + c0:dff + c0 + FFN_COLS], preferred_element_type=F32)
        fext_ref[:, F_HP:F_HP + lt, cs] = fa.reshape(bt, lt, FFN_COLS)
        if bt == 1:
            n = F_HP + lt
            ext = fext_ref[0, :, cs]
            conv = fcb[:, cs]
            for k in range(F_CONV):
                off = F_HP - (F_CONV - 1) + k
                shifted = fa if off == F_HP else pltpu.roll(ext, n - off, axis=0)[0:lt]
                conv = conv + fcw[k:k + 1, cs] * shifted
        else:
            conv = fcb[:, cs][None]
            for k in range(F_CONV):
                off = F_HP - (F_CONV - 1) + k
                conv = conv + fcw[k:k + 1, cs][None] * fext_ref[:, off:off + lt, cs]
        act = jax.nn.gelu(conv.reshape(bt * lt, FFN_COLS)) * fg
        act_ref[:, cs] = act.astype(BF16)
    del fa_ref
    return x2 + jnp.dot(act_ref[...], fwdown_ref[...], preferred_element_type=F32)


def _memkv_kernel(mem_ref, nmem_ref, wk_ref, wv_ref, mk_ref, mv_ref, kb_ref, vb_ref):
    mn = _rms(mem_ref[...], nmem_ref[...]).astype(BF16)
    mk = jnp.dot(mn, wk_ref[...], preferred_element_type=F32)
    mv = jnp.dot(mn, wv_ref[...], preferred_element_type=F32)
    kb_ref[...] = mk.astype(BF16)
    vb_ref[...] = mv.astype(BF16)
    tm, d = mk.shape
    dh = d // X_HEADS
    nblk = dh // LANES
    grp = nblk * X_HEADS
    for j in range(nblk):
        for h in range(X_HEADS):
            cs = slice(h * dh + j * LANES, h * dh + (j + 1) * LANES)
            rows = pl.ds(j * X_HEADS + h, tm, stride=grp)
            mk_ref[rows, :] = mk[:, cs]
            mv_ref[rows, :] = mv[:, cs]


def _const_spec(shape):
    nd = len(shape)
    return pl.BlockSpec(shape, lambda *_: (0,) * nd, pipeline_mode=pl.Buffered(1))


def _memkv(mem2d, nmem, wk, wv):
    m, d = mem2d.shape
    tm = min(512, m)
    grp = d // LANES
    row = pl.BlockSpec((tm, d), lambda i: (i, 0))
    cache_rows = pl.BlockSpec((tm * grp, LANES), lambda i: (i, 0))
    return pl.pallas_call(
        _memkv_kernel,
        grid=(m // tm,),
        in_specs=[row, _const_spec(nmem.shape), _const_spec(wk.shape), _const_spec(wv.shape)],
        out_specs=[cache_rows, cache_rows, row, row],
        out_shape=[jax.ShapeDtypeStruct((m * grp, LANES), F32), jax.ShapeDtypeStruct((m * grp, LANES), F32),
                   jax.ShapeDtypeStruct((m, d), BF16), jax.ShapeDtypeStruct((m, d), BF16)],
        compiler_params=pltpu.CompilerParams(dimension_semantics=("arbitrary",), vmem_limit_bytes=VMEM_LIMIT_BYTES),
        name="memkv",
    )(mem2d, nmem, wk, wv)


def _mlstm_gates(g, m0):
    c = g.shape[0]
    cs = _scan(g, 0, jnp.add, 0.0)
    f = pltpu.roll(cs, LANES - A_HEADS, axis=1)
    a = g - f
    cm = _scan(a, 0, jnp.maximum, -jnp.inf)
    m = f + jnp.maximum(m0, cm)
    f_last = f[c - 1:c, :]
    m_last = m[c - 1:c, :]
    return dict(a=a, dlog=f - m, inter=jnp.exp(f + m0 - m), em=jnp.exp(-m),
                wdec=jnp.exp(f_last + a - m_last), cdec=jnp.exp(f_last + m0 - m_last), m_last=m_last)


def _mlstm_chunk_prompt(rows, q_ref, k_ref, v_ref, g_ref, ao_ref, ha_ref, ahn_ref, c_s, n_s, m_s):
    c = A_CHUNK
    dh = LANES
    gt = _mlstm_gates(g_ref[rows, :], m_s[0:1, :])
    a_t = gt["a"].T
    t_idx = lax.broadcasted_iota(jnp.int32, (c, c), 0)
    s_idx = lax.broadcasted_iota(jnp.int32, (c, c), 1)
    causal = t_idx >= s_idx
    for h in range(A_HEADS):
        hs = slice(h * dh, (h + 1) * dh)
        qh = q_ref[rows, hs]
        kh = k_ref[rows, hs]
        vh = v_ref[rows, hs]
        icol = gt["inter"][:, h:h + 1]
        wcol = gt["wdec"][:, h:h + 1]
        dm = jnp.exp(jnp.where(causal, gt["dlog"][:, h:h + 1] + a_t[h:h + 1, :], -jnp.inf))
        s = _dot_nt(qh, kh) * dm
        c0 = c_s[h]
        n0 = n_s[h:h + 1, :]
        num = _dot(s, vh) + icol * _dot_nt(qh, c0)
        den = jnp.sum(s, axis=-1, keepdims=True) + icol * jnp.sum(qh * n0, axis=-1, keepdims=True)
        hh = num / jnp.maximum(jnp.abs(den), gt["em"][:, h:h + 1])
        cd = gt["cdec"][:, h:h + 1]
        c_s[h] = cd * c0 + _dot((vh * wcol).T, kh)
        n_s[h:h + 1, :] = cd * n0 + jnp.sum(kh * wcol, axis=0, keepdims=True)
        ha_ref[rows, hs] = _head_out(hh, ahn_ref[:, hs], ao_ref[rows, hs])
    m_s[0:1, :] = gt["m_last"]


def _prompt_kernel(
        x_ref, kb_ref, vb_ref,
        nmix_ref, wmain_ref, wg_ref, gbias_ref, acw_ref, acb_ref, wq_ref, wk_ref, wv_ref, ahn_ref,
        bcw_ref, bcb_ref, blg_ref, blb_ref, wout_ref, nx_ref, xwq_ref, xwo_ref,
        nffn_ref, fwup_ref, fcw_ref, fcb_ref, fwdown_ref, nfin_ref,
        y_ref, c_out, n_out, m_out, ac_out, bc_out, fc_out,
        c_s, n_s, m_s, aext, bext, fext, q_s, k_s, v_s, g_s, ao_s, ha_s, cv_s, act_s, rot_s):
    tl = x_ref.shape[1]
    l = pl.program_id(1)
    last = l == pl.num_programs(1) - 1

    @pl.when(l == 0)
    def _():
        c_s[...] = jnp.zeros_like(c_s)
        n_s[...] = jnp.zeros_like(n_s)
        m_s[...] = jnp.zeros_like(m_s)
        aext[:, 0:A_HP, :] = jnp.zeros((1, A_HP, aext.shape[2]), F32)
        bext[:, 0:B_HP, :] = jnp.zeros((1, B_HP, bext.shape[2]), F32)
        fext[:, 0:F_HP, :] = jnp.zeros((1, F_HP, fext.shape[2]), F32)

    x = x_ref[0]
    a_x, a_o, b_u, b_g, g2 = _in_proj(x, nmix_ref, wmain_ref, wg_ref, gbias_ref)
    g_s[...] = g2
    ao_s[...] = a_o

    aext[:, A_HP:A_HP + tl, :] = a_x[None]
    _conv_taps_seq(aext, rot_s, cv_s, acw_ref, acb_ref, A_CONV, A_HP, tl, CONV_ROWS)
    pre = cv_s[0]
    a_c = (pre * _sigmoid(pre)).astype(BF16)
    q_s[...] = jnp.dot(a_c, wq_ref[...], preferred_element_type=F32)
    k_s[...] = jnp.dot(a_c, wk_ref[...], preferred_element_type=F32) * (LANES ** -0.5)
    v_s[...] = jnp.dot(a_x.astype(BF16), wv_ref[...], preferred_element_type=F32)
    a_hist = aext[:, A_HP + tl - (A_CONV - 1):A_HP + tl, :]
    aext[:, A_HP - (A_CONV - 1):A_HP, :] = a_hist
    for j in range(tl // A_CHUNK):
        _mlstm_chunk_prompt(slice(j * A_CHUNK, (j + 1) * A_CHUNK), q_s, k_s, v_s, g_s, ao_s, ha_s, ahn_ref,
                            c_s, n_s, m_s)

    bext[:, B_HP:B_HP + tl, :] = (b_u * _sigmoid(b_g))[None]
    _conv_taps_seq(bext, rot_s, cv_s, bcw_ref, bcb_ref, B_CONV, B_HP, tl, CONV_ROWS)
    cb = _layernorm_silu(cv_s[0], blg_ref[...], blb_ref[...])
    b_hist = bext[:, B_HP + tl - (B_CONV - 1):B_HP + tl, :]
    bext[:, B_HP - (B_CONV - 1):B_HP, :] = b_hist
    d_a = ha_s.shape[1]
    x1 = x + _dot(ha_s[...], wout_ref[0:d_a, :]) + _dot(cb, wout_ref[d_a:, :])

    qx = _dot(_rms(x1, nx_ref[...]), xwq_ref[...])
    dh = qx.shape[1] // X_HEADS
    ox = _attention(qx, lambda h: kb_ref[0, :, h * dh:(h + 1) * dh], lambda h: vb_ref[0, :, h * dh:(h + 1) * dh])
    x2 = x1 + _dot(ox, xwo_ref[...])

    x3 = _ffn(x2, fext, None, act_s, nffn_ref, fwup_ref, fcw_ref, fcb_ref, fwdown_ref, tl)
    f_hist = fext[:, F_HP + tl - (F_CONV - 1):F_HP + tl, :]
    fext[:, F_HP - (F_CONV - 1):F_HP, :] = f_hist
    y_ref[0] = _rms(x3, nfin_ref[...])

    @pl.when(last)
    def _():
        c_out[0] = c_s[...]
        n_out[0] = n_s[0:A_HEADS, :]
        m_out[0] = m_s[...]
        ac_out[...] = a_hist
        bc_out[...] = b_hist
        fc_out[...] = f_hist


def _prompt_layer(x, kb, vb, w):
    b, l, d = x.shape
    tl = PROMPT_TILE
    n_mem = kb.shape[1]
    d_a = w["wq"].shape[0]
    d_ff = w["fwdown"].shape[0]
    names = ["nmix", "wmain", "wg", "gbias", "acw", "acb", "wq", "wk", "wv", "ahn", "bcw", "bcb", "blg", "blb",
             "wout", "nx", "xwq", "xwo", "nffn", "fwup", "fcw", "fcb", "fwdown", "nfin"]
    weights = [w[n] for n in names]
    per_b3 = lambda shape: pl.BlockSpec((1,) + shape, lambda i, j: (i, 0, 0))
    in_specs = [pl.BlockSpec((1, tl, d), lambda i, j: (i, j, 0)), per_b3((n_mem, d)), per_b3((n_mem, d))]
    in_specs += [_const_spec(a.shape) for a in weights]
    out_specs = [
        pl.BlockSpec((1, tl, d), lambda i, j: (i, j, 0)),
        pl.BlockSpec((1, A_HEADS, LANES, LANES), lambda i, j: (i, 0, 0, 0)),
        per_b3((A_HEADS, LANES)),
        per_b3((SUBLANES, LANES)),
        per_b3((A_CONV - 1, d_a)),
        per_b3((B_CONV - 1, d_a)),
        per_b3((F_CONV - 1, d_ff)),
    ]
    out_shape = [
        jax.ShapeDtypeStruct((b, l, d), F32),
        jax.ShapeDtypeStruct((b, A_HEADS, LANES, LANES), F32),
        jax.ShapeDtypeStruct((b, A_HEADS, LANES), F32),
        jax.ShapeDtypeStruct((b, SUBLANES, LANES), F32),
        jax.ShapeDtypeStruct((b, A_CONV - 1, d_a), F32),
        jax.ShapeDtypeStruct((b, B_CONV - 1, d_a), F32),
        jax.ShapeDtypeStruct((b, F_CONV - 1, d_ff), F32),
    ]
    scratch = [
        pltpu.VMEM((A_HEADS, LANES, LANES), F32),
        pltpu.VMEM((SUBLANES, LANES), F32),
        pltpu.VMEM((SUBLANES, LANES), F32),
        pltpu.VMEM((1, A_HP + tl, d_a), F32),
        pltpu.VMEM((1, B_HP + tl, d_a), F32),
        pltpu.VMEM((1, F_HP + tl, d_ff), F32),
        pltpu.VMEM((tl, d_a), F32),
        pltpu.VMEM((tl, d_a), F32),
        pltpu.VMEM((tl, d_a), F32),
        pltpu.VMEM((tl, LANES), F32),
        pltpu.VMEM((tl, d_a), F32),
        pltpu.VMEM((tl, d_a), F32),
        pltpu.VMEM((1, tl, d_a), F32),
        pltpu.VMEM((tl, d_ff), BF16),
        pltpu.VMEM((SUBLANES, B_HP + tl, d_a), F32),
    ]
    return pl.pallas_call(
        _prompt_kernel,
        grid=(b, l // tl),
        in_specs=in_specs,
        out_specs=out_specs,
        out_shape=out_shape,
        scratch_shapes=scratch,
        compiler_params=pltpu.CompilerParams(dimension_semantics=("arbitrary", "arbitrary"),
                                             vmem_limit_bytes=VMEM_LIMIT_BYTES),
        name="prompt_layer",
    )(x, kb, vb, *weights)


def _mlstm_seq_sample(b, lt, q_ref, k_ref, v_ref, g_ref, ao_ref, ha_ref, ahn_ref, c0_ref, n0_ref, m0_ref,
                      c1_ref, n1_ref, m1_ref):
    dh = LANES
    rows = pl.ds(pl.multiple_of(b * lt, lt), lt)
    gt = _mlstm_gates(g_ref[rows, :], m0_ref[b])
    a, dlog, inter, em, wdec, cdec = (gt[n] for n in ("a", "dlog", "inter", "em", "wdec", "cdec"))
    m1_ref[b] = gt["m_last"]
    pad = jnp.zeros((LANES - lt, LANES), F32)
    a_t = jnp.concatenate([a, pad], axis=0).T
    t_idx = lax.broadcasted_iota(jnp.int32, (lt, LANES), 0)
    s_idx = lax.broadcasted_iota(jnp.int32, (lt, LANES), 1)
    causal = t_idx >= s_idx
    for h in range(A_HEADS):
        hs = slice(h * dh, (h + 1) * dh)
        qh = q_ref[rows, hs]
        kh = k_ref[rows, hs]
        vh = v_ref[rows, hs]
        kpad = jnp.concatenate([kh, pad], axis=0)
        vpad = jnp.concatenate([vh, pad], axis=0)
        dm = jnp.exp(jnp.where(causal, dlog[:, h:h + 1] + a_t[h:h + 1, :], -jnp.inf))
        s = _dot_nt(qh, kpad) * dm
        c0 = c0_ref[b, h]
        n0 = n0_ref[b, h:h + 1, :]
        icol = inter[:, h:h + 1]
        num = _dot(s, vpad) + icol * _dot_nt(qh, c0)
        den = jnp.sum(s, axis=-1, keepdims=True) + icol * jnp.sum(qh * n0, axis=-1, keepdims=True)
        hh = num / jnp.maximum(jnp.abs(den), em[:, h:h + 1])
        cd = cdec[:, h:h + 1]
        wcol = wdec[:, h:h + 1]
        vw_t = jnp.concatenate([vh * wcol, pad], axis=0).T
        c1_ref[b, h] = cd * c0 + _dot(vw_t, kpad)
        n1_ref[b, h:h + 1, :] = cd * n0 + jnp.sum(kh * wcol, axis=0, keepdims=True)
        ha_ref[rows, hs] = _head_out(hh, ahn_ref[:, hs], ao_ref[rows, hs])


def _sample_mix_kernel(
        x_ref, c0_ref, n0_ref, m0_ref, ac_in, bc_in,
        nmix_ref, wmain_ref, wg_ref, gbias_ref, acw_ref, acb_ref, wq_ref, wk_ref, wv_ref, ahn_ref,
        bcw_ref, bcb_ref, blg_ref, blb_ref, wout_ref, nx_ref, xwq_ref,
        x1_ref, qx_ref, c1_ref, n1_ref, m1_ref, ac_out, bc_out,
        aext, bext, q_s, k_s, v_s, g_s, ao_s, ha_s, cva_s, cvb_s):
    bt = c0_ref.shape[0]
    rows = x_ref.shape[0]
    lt = rows // bt
    d_a = ha_s.shape[1]
    x = x_ref[...]
    a_x, a_o, b_u, b_g, g2 = _in_proj(x, nmix_ref, wmain_ref, wg_ref, gbias_ref)
    g_s[...] = g2
    ao_s[...] = a_o

    aext[:, A_HP - (A_CONV - 1):A_HP, :] = ac_in[...]
    aext[:, A_HP:A_HP + lt, :] = a_x.reshape(bt, lt, d_a)
    _conv_taps(aext, cva_s, acw_ref, acb_ref, A_CONV, A_HP, lt, CONV_ROWS // lt, lt)
    ac_out[...] = aext[:, A_HP + lt - (A_CONV - 1):A_HP + lt, :]
    pre = cva_s[...].reshape(rows, d_a)
    a_c = (pre * _sigmoid(pre)).astype(BF16)
    q_s[...] = jnp.dot(a_c, wq_ref[...], preferred_element_type=F32)
    k_s[...] = jnp.dot(a_c, wk_ref[...], preferred_element_type=F32) * (LANES ** -0.5)
    v_s[...] = jnp.dot(a_x.astype(BF16), wv_ref[...], preferred_element_type=F32)

    def seq(b, carry):
        _mlstm_seq_sample(b, lt, q_s, k_s, v_s, g_s, ao_s, ha_s, ahn_ref, c0_ref, n0_ref, m0_ref,
                          c1_ref, n1_ref, m1_ref)
        return carry
    lax.fori_loop(0, bt, seq, 0, unroll=4)

    bext[:, B_HP - (B_CONV - 1):B_HP, :] = bc_in[...]
    bext[:, B_HP:B_HP + lt, :] = (b_u * _sigmoid(b_g)).reshape(bt, lt, d_a)
    _conv_taps(bext, cvb_s, bcw_ref, bcb_ref, B_CONV, B_HP, lt, CONV_ROWS // lt, lt)
    bc_out[...] = bext[:, B_HP + lt - (B_CONV - 1):B_HP + lt, :]
    cb = _layernorm_silu(cvb_s[...].reshape(rows, d_a), blg_ref[...], blb_ref[...])
    x1 = x + _dot(ha_s[...], wout_ref[0:d_a, :]) + _dot(cb, wout_ref[d_a:, :])
    x1_ref[...] = x1
    qx_ref[...] = _dot(_rms(x1, nx_ref[...]), xwq_ref[...])


def _sample_mix(x2d, c0, n0, m0p, ac0, bc0, w, bt):
    rows_total, d = x2d.shape
    nb = c0.shape[0]
    lt = rows_total // nb
    rows = bt * lt
    d_a = w["wq"].shape[0]
    names = ["nmix", "wmain", "wg", "gbias", "acw", "acb", "wq", "wk", "wv", "ahn", "bcw", "bcb", "blg", "blb",
             "wout", "nx", "xwq"]
    weights = [w[n] for n in names]
    row_spec = pl.BlockSpec((rows, d), lambda i: (i, 0))
    b3 = lambda shape: pl.BlockSpec((bt,) + shape, lambda i: (i, 0, 0))
    c_spec = pl.BlockSpec((bt, A_HEADS, LANES, LANES), lambda i: (i, 0, 0, 0))
    in_specs = [row_spec, c_spec, b3((A_HEADS, LANES)), b3((1, LANES)), b3((A_CONV - 1, d_a)), b3((B_CONV - 1, d_a))]
    in_specs += [_const_spec(a.shape) for a in weights]
    out_specs = [row_spec, row_spec, c_spec, b3((A_HEADS, LANES)), b3((1, LANES)),
                 b3((A_CONV - 1, d_a)), b3((B_CONV - 1, d_a))]
    out_shape = [
        jax.ShapeDtypeStruct((rows_total, d), F32),
        jax.ShapeDtypeStruct((rows_total, d), F32),
        jax.ShapeDtypeStruct(c0.shape, F32),
        jax.ShapeDtypeStruct(n0.shape, F32),
        jax.ShapeDtypeStruct(m0p.shape, F32),
        jax.ShapeDtypeStruct(ac0.shape, F32),
        jax.ShapeDtypeStruct(bc0.shape, F32),
    ]
    scratch = [
        pltpu.VMEM((bt, A_HP + lt, d_a), F32),
        pltpu.VMEM((bt, B_HP + lt, d_a), F32),
        pltpu.VMEM((rows, d_a), F32),
        pltpu.VMEM((rows, d_a), F32),
        pltpu.VMEM((rows, d_a), F32),
        pltpu.VMEM((rows, LANES), F32),
        pltpu.VMEM((rows, d_a), F32),
        pltpu.VMEM((rows, d_a), F32),
        pltpu.VMEM((bt, lt, d_a), F32),
        pltpu.VMEM((bt, lt, d_a), F32),
    ]
    return pl.pallas_call(
        _sample_mix_kernel,
        grid=(nb // bt,),
        in_specs=in_specs,
        out_specs=out_specs,
        out_shape=out_shape,
        scratch_shapes=scratch,
        compiler_params=pltpu.CompilerParams(dimension_semantics=("arbitrary",), vmem_limit_bytes=VMEM_LIMIT_BYTES),
        name="sample_mix",
    )(x2d, c0, n0, m0p, ac0, bc0, *weights)


def _sample_attn_kernel(qx_ref, k_ref, v_ref, ox_ref):
    bt, nrow, _ = k_ref.shape
    lt = qx_ref.shape[0] // bt
    dh = qx_ref.shape[1] // X_HEADS
    nblk = dh // LANES
    grp = nblk * X_HEADS
    assert nblk == 2 and grp == SUBLANES and lt == SUBLANES
    nq = grp * lt
    scale = dh ** -0.5
    row_g = lax.broadcasted_iota(jnp.int32, (nq, LANES), 0) // lt
    col_g = lax.broadcasted_iota(jnp.int32, (nq, LANES), 1) % grp
    mask = row_g == col_g
    upper = col_g >= X_HEADS
    lane_blocks = [(j, h) for j in range(nblk) for h in range(X_HEADS)]

    def one(b, carry):
        r0 = pl.multiple_of(b * lt, lt)
        qx = qx_ref[pl.ds(r0, lt), :]
        qrows = jnp.concatenate([qx[:, h * dh + j * LANES:h * dh + (j + 1) * LANES] for j, h in lane_blocks], axis=0)
        g = _dot_nt(qrows, k_ref[b])
        scores = []
        for c in range(nrow // LANES):
            e = jnp.where(mask, g[:, c * LANES:(c + 1) * LANES], 0.0)
            other = jnp.concatenate([e[nq // 2:], e[:nq // 2]], axis=0)
            other = jnp.where(upper, pltpu.roll(other, X_HEADS, axis=1), pltpu.roll(other, LANES - X_HEADS, axis=1))
            scores.append(jnp.where(mask, (e + other) * scale, -jnp.inf))
        mx = scores[0]
        for s in scores[1:]:
            mx = jnp.maximum(mx, s)
        mx = jnp.max(mx, axis=-1, keepdims=True)
        es = [jnp.exp(s - mx) for s in scores]
        tot = es[0]
        for e in es[1:]:
            tot = tot + e
        inv = 1.0 / jnp.sum(tot, axis=-1, keepdims=True)
        p = jnp.concatenate([e * inv for e in es], axis=1)
        o = _dot(p, v_ref[b])
        for i, (j, h) in enumerate(lane_blocks):
            ox_ref[pl.ds(r0, lt), h * dh + j * LANES:h * dh + (j + 1) * LANES] = o[i * lt:(i + 1) * lt, :]
        return carry

    lax.fori_loop(0, bt, one, 0, unroll=2)


def _cache_rows(mem):
    nb, n_mem, xh, dh = mem.shape
    nblk = dh // LANES
    return mem.reshape(nb, n_mem, xh, nblk, LANES).transpose(0, 1, 3, 2, 4).reshape(nb, n_mem * nblk * xh, LANES)


def _sample_attn(qx2d, mem_k, mem_v, bt):
    rows_total, d = qx2d.shape
    nb, nrow, _ = mem_k.shape
    lt = rows_total // nb
    row_spec = pl.BlockSpec((bt * lt, d), lambda i: (i, 0))
    kv_spec = pl.BlockSpec((bt, nrow, LANES), lambda i: (i, 0, 0))
    return pl.pallas_call(
        _sample_attn_kernel,
        grid=(nb // bt,),
        in_specs=[row_spec, kv_spec, kv_spec],
        out_specs=row_spec,
        out_shape=jax.ShapeDtypeStruct((rows_total, d), F32),
        compiler_params=pltpu.CompilerParams(dimension_semantics=("arbitrary",), vmem_limit_bytes=VMEM_LIMIT_BYTES),
        name="sample_attn",
    )(qx2d, mem_k, mem_v)


def _sample_ffn_kernel(x1_ref, ox_ref, fc_in, xwo_ref, nffn_ref, fwup_ref, fcw_ref, fcb_ref, fwdown_ref, nfin_ref,
                       y_ref, fc_out, fext, act_s):
    bt = fc_in.shape[0]
    lt = x1_ref.shape[0] // bt
    x2 = x1_ref[...] + _dot(ox_ref[...], xwo_ref[...])
    fext[:, F_HP - (F_CONV - 1):F_HP, :] = fc_in[...]
    x3 = _ffn(x2, fext, None, act_s, nffn_ref, fwup_ref, fcw_ref, fcb_ref, fwdown_ref, lt)
    fc_out[...] = fext[:, F_HP + lt - (F_CONV - 1):F_HP + lt, :]
    y_ref[...] = _rms(x3, nfin_ref[...])


def _sample_ffn(x1, ox, fc0, w, bt):
    rows_total, d = x1.shape
    nb = fc0.shape[0]
    lt = rows_total // nb
    rows = bt * lt
    d_ff = w["fwdown"].shape[0]
    names = ["xwo", "nffn", "fwup", "fcw", "fcb", "fwdown", "nfin"]
    weights = [w[n] for n in names]
    row_spec = pl.BlockSpec((rows, d), lambda i: (i, 0))
    st_spec = pl.BlockSpec((bt, F_CONV - 1, d_ff), lambda i: (i, 0, 0))
    return pl.pallas_call(
        _sample_ffn_kernel,
        grid=(nb // bt,),
        in_specs=[row_spec, row_spec, st_spec] + [_const_spec(a.shape) for a in weights],
        out_specs=[row_spec, st_spec],
        out_shape=[jax.ShapeDtypeStruct((rows_total, d), F32), jax.ShapeDtypeStruct(fc0.shape, F32)],
        scratch_shapes=[pltpu.VMEM((bt, F_HP + lt, d_ff), F32), pltpu.VMEM((rows, d_ff), BF16)],
        compiler_params=pltpu.CompilerParams(dimension_semantics=("arbitrary",), vmem_limit_bytes=VMEM_LIMIT_BYTES),
        name="sample_ffn",
    )(x1, ox, fc0, *weights)


def _block_diag(wh):
    h, d, _ = wh.shape
    eye = jnp.eye(h, dtype=wh.dtype)
    return (eye[:, None, :, None] * wh[:, :, None, :]).reshape(h * d, h * d)


def _layer_weights(l, norm_mix, w_in, a_conv_w, a_conv_b, a_wq, a_wk, a_wv, a_bi, a_bf, a_hnorm, b_conv_w, b_conv_b,
                   b_ln_g, b_ln_b, w_out, norm_x, x_wq, x_wo, norm_ffn, f_wup, f_conv_w, f_conv_b, f_wdown,
                   norm_final):
    d_a = a_conv_w.shape[2]
    g0 = 2 * d_a
    g1 = g0 + 2 * A_HEADS
    row = lambda v: v.reshape(1, -1).astype(F32)
    wi = w_in[l]
    gbias = jnp.zeros((LANES,), F32).at[0:A_HEADS].set(a_bi[l]).at[A_HEADS:2 * A_HEADS].set(a_bf[l])
    return dict(
        nmix=row(norm_mix[l]),
        wmain=jnp.concatenate([wi[:, :g0], wi[:, g1:]], axis=1).astype(BF16),
        wg=jnp.pad(wi[:, g0:g1], ((0, 0), (0, LANES - 2 * A_HEADS))).astype(BF16),
        gbias=row(gbias),
        acw=a_conv_w[l], acb=row(a_conv_b[l]),
        wq=_block_diag(a_wq[l]).astype(BF16), wk=_block_diag(a_wk[l]).astype(BF16),
        wv=_block_diag(a_wv[l]).astype(BF16),
        ahn=row(a_hnorm[l]),
        bcw=b_conv_w[l], bcb=row(b_conv_b[l]), blg=row(b_ln_g[l]), blb=row(b_ln_b[l]),
        wout=w_out[l].astype(BF16), nx=row(norm_x[l]), xwq=x_wq[l].astype(BF16), xwo=x_wo[l].astype(BF16),
        nffn=row(norm_ffn[l]), fwup=f_wup[l].astype(BF16), fcw=f_conv_w[l], fcb=row(f_conv_b[l]),
        fwdown=f_wdown[l].astype(BF16), nfin=row(norm_final),
    )


def kernel(x_prompt, x_sample, mem_prompt, state_mlstm_C, state_mlstm_n, state_mlstm_m, state_mlstm_conv, state_conv, state_ffn_conv, cache_mem_k, cache_mem_v, norm_mix, w_in, a_conv_w, a_conv_b, a_wq, a_wk, a_wv, a_bi, a_bf, a_hnorm, b_conv_w, b_conv_b, b_ln_g, b_ln_b, w_out, norm_x, norm_mem, x_wq, x_wk, x_wv, x_wo, norm_ffn, f_wup, f_conv_w, f_conv_b, f_wdown, norm_final):
    depth = w_in.shape[0]
    assert depth == 1, "the final norm is fused into the (single) layer"
    bp, lp, d = x_prompt.shape
    bs, ls, _ = x_sample.shape
    n_mem = mem_prompt.shape[1]
    l = 0
    w = _layer_weights(l, norm_mix, w_in, a_conv_w, a_conv_b, a_wq, a_wk, a_wv, a_bi, a_bf, a_hnorm, b_conv_w,
                       b_conv_b, b_ln_g, b_ln_b, w_out, norm_x, x_wq, x_wo, norm_ffn, f_wup, f_conv_w, f_conv_b,
                       f_wdown, norm_final)

    mk, mv, kb, vb = _memkv(mem_prompt.reshape(bp * n_mem, d), norm_mem[l].reshape(1, d),
                            x_wk[l].astype(BF16), x_wv[l].astype(BF16))
    y_p, c_p, n_p, m_p, ac_p, bc_p, fc_p = _prompt_layer(
        x_prompt, kb.reshape(bp, n_mem, d), vb.reshape(bp, n_mem, d), w)
    xh = cache_mem_k.shape[3]
    from_rows = lambda t: t.reshape(bp, n_mem, d // xh // LANES, xh, LANES).transpose(0, 1, 3, 2, 4).reshape(
        1, bp, n_mem, xh, d // xh)
    memk_p = from_rows(mk)
    memv_p = from_rows(mv)

    m0p = jnp.pad(state_mlstm_m[l], ((0, 0), (0, LANES - A_HEADS)))[:, None, :]
    x1, qx, c_s, n_s, m_s, ac_s, bc_s = _sample_mix(
        x_sample.reshape(bs * ls, d), state_mlstm_C[l], state_mlstm_n[l], m0p, state_mlstm_conv[l], state_conv[l],
        w, bt=16)
    ox = _sample_attn(qx, _cache_rows(cache_mem_k[l]), _cache_rows(cache_mem_v[l]), bt=8)
    y_s, fc_s = _sample_ffn(x1, ox, state_ffn_conv[l], w, bt=32)

    return (y_p, y_s.reshape(bs, ls, d),
            c_p[None], n_p[None], m_p[None, :, 0, 0:A_HEADS], ac_p[None], bc_p[None], fc_p[None], memk_p, memv_p,
            c_s[None], n_s[None], m_s[None, :, 0, 0:A_HEADS], ac_s[None], bc_s[None], fc_s[None])
```
